```python
import jax, jax.numpy as jnp
from jax import lax
import numpy as np

D_MODEL = 2048
BATCH = 1
SEQ = 16384
DEPTH = 1

D_MIX = D_MODEL
LRU_WIDTH = D_MIX // 2
SC_WIDTH = D_MIX - LRU_WIDTH
LRU_HEADS = 16
LRU_HEAD_DIM = LRU_WIDTH // LRU_HEADS
SC_GROUPS = 16
LRU_CONV_K = 4
LRU_CONV_PAD_LEFT = 2
SC_CONV_K = 3
SC_CONV_PAD_LEFT = 1
LRU_C = 8.0
PROJ_COLS = 2 * LRU_WIDTH + 4 * SC_WIDTH
DEEPNORM_ALPHA = (2.0 * DEPTH) ** 0.25
DEEPNORM_BETA = (8.0 * DEPTH) ** -0.25
LN_EPS = 1e-5
RMS_EPS = 1e-6

kernel_name = "hybrid_rglru_shortconv_deepnorm_encoder"


def depthwise_conv(u, w, b, pad_left):
    k_w = w.shape[0]
    s = u.shape[1]
    up = jnp.pad(u, ((0, 0), (pad_left, k_w - 1 - pad_left), (0, 0)))
    out = up[:, 0:s, :] * w[0]
    for k in range(1, k_w):
        out = out + up[:, k:k + s, :] * w[k]
    if b is not None:
        out = out + b
    return out


def block_diag_linear(u, w, b):
    bsz, s, _ = u.shape
    uh = u.reshape(bsz, s, LRU_HEADS, LRU_HEAD_DIM)
    y = jnp.einsum('bshi,hij->bshj', uh, w) + b
    return y.reshape(bsz, s, LRU_HEADS * LRU_HEAD_DIM)


def rg_lru(u, w_a, b_a, w_x, b_x, lam, reverse):
    u32 = u.astype(jnp.float32)
    r = jax.nn.sigmoid(block_diag_linear(u32, w_a.astype(jnp.float32), b_a.astype(jnp.float32)))
    i = jax.nn.sigmoid(block_diag_linear(u32, w_x.astype(jnp.float32), b_x.astype(jnp.float32)))
    log_a = -LRU_C * r * jax.nn.softplus(-lam.astype(jnp.float32))
    a = jnp.exp(log_a)
    mult = jnp.sqrt(-jnp.expm1(2.0 * log_a))
    bvals = mult * (i * u32)

    def combine(left, right):
        a_l, b_l = left
        a_r, b_r = right
        return a_l * a_r, a_r * b_l + b_r

    _, h = lax.associative_scan(combine, (a, bvals), reverse=reverse, axis=1)
    return h


def group_rmsnorm(y, g):
    y32 = y.astype(jnp.float32)
    ms = jnp.mean(y32 * y32, axis=-1, keepdims=True)
    return y32 * lax.rsqrt(ms + RMS_EPS) * g.astype(jnp.float32)


def layernorm(y, g, b):
    y32 = y.astype(jnp.float32)
    mu = jnp.mean(y32, axis=-1, keepdims=True)
    var = jnp.mean(jnp.square(y32 - mu), axis=-1, keepdims=True)
    return (y32 - mu) * lax.rsqrt(var + LN_EPS) * g.astype(jnp.float32) + b.astype(jnp.float32)


def setup_inputs(seed: int = 0) -> dict:
    key = jax.random.key(seed)
    ks = jax.random.split(key, 24)
    f32 = jnp.float32
    L = DEPTH

    def lam_init(k):
        ac = jax.random.uniform(k, (L, LRU_WIDTH), f32, 0.9, 0.999)
        a = ac ** (1.0 / LRU_C)
        return jnp.log(a) - jnp.log1p(-a)

    bd_scale = LRU_HEAD_DIM ** -0.5
    return {
        "x": jax.random.normal(ks[0], (BATCH, SEQ, D_MODEL), f32),
        "w_in": jax.random.normal(ks[1], (L, D_MODEL, PROJ_COLS), f32) * D_MODEL ** -0.5,
        "lru_conv_w": jax.random.normal(ks[2], (L, LRU_CONV_K, LRU_WIDTH), f32) * LRU_CONV_K ** -0.5,
        "lru_conv_b": jax.random.normal(ks[3], (L, LRU_WIDTH), f32) * 0.01,
        "lru_wa_f": jax.random.normal(ks[4], (L, LRU_HEADS, LRU_HEAD_DIM, LRU_HEAD_DIM), f32) * bd_scale,
        "lru_ba_f": jax.random.normal(ks[5], (L, LRU_HEADS, LRU_HEAD_DIM), f32) * 0.01,
        "lru_wx_f": jax.random.normal(ks[6], (L, LRU_HEADS, LRU_HEAD_DIM, LRU_HEAD_DIM), f32) * bd_scale,
        "lru_bx_f": jax.random.normal(ks[7], (L, LRU_HEADS, LRU_HEAD_DIM), f32) * 0.01,
        "lru_lam_f": lam_init(ks[8]),
        "lru_wa_b": jax.random.normal(ks[9], (L, LRU_HEADS, LRU_HEAD_DIM, LRU_HEAD_DIM), f32) * bd_scale,
        "lru_ba_b": jax.random.normal(ks[10], (L, LRU_HEADS, LRU_HEAD_DIM), f32) * 0.01,
        "lru_wx_b": jax.random.normal(ks[11], (L, LRU_HEADS, LRU_HEAD_DIM, LRU_HEAD_DIM), f32) * bd_scale,
        "lru_bx_b": jax.random.normal(ks[12], (L, LRU_HEADS, LRU_HEAD_DIM), f32) * 0.01,
        "lru_lam_b": lam_init(ks[13]),
        "sc_conv_w": jax.random.normal(ks[14], (L, SC_CONV_K, SC_WIDTH), f32) * SC_CONV_K ** -0.5,
        "lru_norm_g": 1.0 + 0.02 * jax.random.normal(ks[15], (L, LRU_WIDTH), f32),
        "sc_norm_g": 1.0 + 0.02 * jax.random.normal(ks[16], (L, SC_WIDTH), f32),
        "w_out": jax.random.normal(ks[17], (L, D_MIX, D_MODEL), f32) * (D_MIX ** -0.5) * DEEPNORM_BETA,
        "ln_g": 1.0 + 0.02 * jax.random.normal(ks[18], (L, D_MODEL), f32),
        "ln_b": 0.02 * jax.random.normal(ks[19], (L, D_MODEL), f32),
    }


def reference(x, w_in, lru_conv_w, lru_conv_b, lru_wa_f, lru_ba_f, lru_wx_f, lru_bx_f, lru_lam_f,
              lru_wa_b, lru_ba_b, lru_wx_b, lru_bx_b, lru_lam_b, sc_conv_w, lru_norm_g, sc_norm_g,
              w_out, ln_g, ln_b):
    dtype = x.dtype
    o1 = LRU_WIDTH
    o2 = o1 + LRU_WIDTH
    o3 = o2 + SC_WIDTH
    o4 = o3 + SC_WIDTH
    o5 = o4 + SC_WIDTH
    for l in range(DEPTH):
        proj = jnp.einsum('bsd,dc->bsc', x, w_in[l])
        lru_x = proj[..., :o1]
        lru_gate = proj[..., o1:o2]
        sc_b = proj[..., o2:o3]
        sc_c = proj[..., o3:o4]
        sc_h = proj[..., o4:o5]
        sc_gate = proj[..., o5:]

        u = depthwise_conv(lru_x, lru_conv_w[l], lru_conv_b[l], LRU_CONV_PAD_LEFT)
        h = (rg_lru(u, lru_wa_f[l], lru_ba_f[l], lru_wx_f[l], lru_bx_f[l], lru_lam_f[l], reverse=False)
             + rg_lru(u, lru_wa_b[l], lru_ba_b[l], lru_wx_b[l], lru_bx_b[l], lru_lam_b[l], reverse=True))
        y_lru = group_rmsnorm(h * jax.nn.silu(lru_gate.astype(jnp.float32)), lru_norm_g[l])

        v = depthwise_conv(sc_c * sc_h, sc_conv_w[l], None, SC_CONV_PAD_LEFT)
        y_sc = group_rmsnorm(sc_b * v * jax.nn.silu(sc_gate), sc_norm_g[l])

        y = jnp.concatenate([y_lru, y_sc], axis=-1).astype(dtype)
        mixed = jnp.einsum('bsc,cd->bsd', y, w_out[l])

        x = layernorm(DEEPNORM_ALPHA * x + mixed, ln_g[l], ln_b[l]).astype(dtype)
    return x
```

```python
import functools

import jax
import jax.numpy as jnp
from jax import lax
from jax.experimental import pallas as pl
from jax.experimental.pallas import tpu as pltpu

D_MODEL = 2048
WIDTH = 1024
HEADS = 16
HEAD_DIM = WIDTH // HEADS
PROJ_COLS = 6 * WIDTH
LRU_C = 8.0
DEEPNORM_ALPHA = 2.0 ** 0.25
LN_EPS = 1e-5
RMS_EPS = 1e-6

SUBLANES = 8
GATE_TILE = 256
N_GATE_TILES = WIDTH // GATE_TILE
TS = 256
HALO = 16
LEAD = SUBLANES
RB = 32
VMEM_LIMIT = 56 * 1024 * 1024

C_LX, C_LG, C_SB, C_SC, C_SH, C_SG = (k * WIDTH for k in range(6))


def _sigmoid(z):
    return 0.5 * jnp.tanh(0.5 * z) + 0.5


def _silu(z):
    return z * _sigmoid(z)


def _c_softplus(lam):
    sp = jnp.maximum(-lam, 0.0) + jnp.log1p(jnp.exp(-jnp.abs(lam)))
    return LRU_C * sp


def _gate_terms(u, za, zx, cpos):
    r = _sigmoid(za)
    i = _sigmoid(zx)
    neg_log_a = r * cpos
    a = jnp.exp(-neg_log_a)
    mult = jnp.sqrt(jnp.tanh(neg_log_a) * (1.0 + a * a))
    return a, mult * (i * u)


def _scan8(a, b, carry, reverse):
    row = lax.broadcasted_iota(jnp.int32, a.shape, 0)
    for s in (1, 2, 4):
        if reverse:
            shift, m = SUBLANES - s, row < SUBLANES - s
        else:
            shift, m = s, row >= s
        a_sh = jnp.where(m, pltpu.roll(a, shift, 0), 1.0)
        b_sh = jnp.where(m, pltpu.roll(b, shift, 0), 0.0)
        b = a * b_sh + b
        a = a * a_sh
    return a * carry + b


def _scan_rows(a, b, carry, reverse):
    groups = list(range(RB // SUBLANES))
    if reverse:
        groups = groups[::-1]
    out = [None] * len(groups)
    for gi in groups:
        sl = slice(gi * SUBLANES, (gi + 1) * SUBLANES)
        h = _scan8(a[sl], b[sl], carry, reverse)
        edge = h[0:1] if reverse else h[SUBLANES - 1:SUBLANES]
        carry = jnp.broadcast_to(edge, h.shape)
        out[gi] = h
    return jnp.concatenate(out, axis=0), carry


def _gate_preacts(ub_ref, wg_ref, za_ref, zx_ref):
    for q in range(N_GATE_TILES):
        cs = slice(q * GATE_TILE, (q + 1) * GATE_TILE)
        z = jnp.dot(ub_ref[:, cs], wg_ref[q], preferred_element_type=jnp.float32)
        za_ref[:, cs] = z[:, :GATE_TILE]
        zx_ref[:, cs] = z[:, GATE_TILE:]


def _pass1_kernel(x_ref, xn_ref, w_ref, cw_ref, cb_ref, wg_ref, ba_ref, bx_ref, lam_ref,
                  scw_ref, scg_ref,
                  u_out, g_out, hf_out, ysc_out,
                  xb, proj, ch, ub, za, zx, hc):
    i = pl.program_id(0)
    last = pl.num_programs(0) - 1

    @pl.when(i == 0)
    def _():
        proj[0:LEAD, C_LX:C_LX + WIDTH] = jnp.zeros((LEAD, WIDTH), jnp.float32)
        ch[0:LEAD, :] = jnp.zeros((LEAD, WIDTH), jnp.float32)
        hc[...] = jnp.zeros_like(hc)

    @pl.when(i > 0)
    def _():
        proj[0:LEAD, C_LX:C_LX + WIDTH] = proj[TS:TS + LEAD, C_LX:C_LX + WIDTH]
        ch[0:LEAD, :] = ch[TS:TS + LEAD, :]

    keep = jnp.where(i == last, 0.0, 1.0)
    xb[0:TS, :] = x_ref[...].astype(jnp.bfloat16)
    xb[TS:TS + HALO, :] = (xn_ref[...] * keep).astype(jnp.bfloat16)

    proj[LEAD:, :] = jnp.dot(xb[...], w_ref[...], preferred_element_type=jnp.float32)

    def ch_body(k, c):
        r0 = pl.multiple_of(k * SUBLANES, SUBLANES) + LEAD
        rows = pl.ds(r0, SUBLANES)
        ch[rows, :] = proj[rows, C_SC:C_SC + WIDTH] * proj[rows, C_SH:C_SH + WIDTH]
        return c

    lax.fori_loop(0, (TS + HALO) // SUBLANES, ch_body, 0)

    def conv_body(k, c):
        r0 = pl.multiple_of(k * RB, RB)
        rows = pl.ds(r0, RB)
        prow = pl.ds(r0 + LEAD, RB)

        lx_win = proj[pl.ds(r0, RB + 2 * LEAD), C_LX:C_LX + WIDTH]
        ch_win = ch[pl.ds(r0, RB + 2 * LEAD), :]

        def tap(win, off):
            return win[LEAD + off:LEAD + off + RB]

        u = (tap(lx_win, -2) * cw_ref[0:1, :] + tap(lx_win, -1) * cw_ref[1:2, :]
             + tap(lx_win, 0) * cw_ref[2:3, :] + tap(lx_win, 1) * cw_ref[3:4, :]
             + cb_ref[...])
        u_out[rows, :] = u
        ub[rows, :] = u.astype(jnp.bfloat16)
        g_out[rows, :] = _silu(proj[prow, C_LG:C_LG + WIDTH])

        v = (tap(ch_win, -1) * scw_ref[0:1, :] + tap(ch_win, 0) * scw_ref[1:2, :]
             + tap(ch_win, 1) * scw_ref[2:3, :])
        y = proj[prow, C_SB:C_SB + WIDTH] * v * _silu(proj[prow, C_SG:C_SG + WIDTH])
        ms = jnp.mean(y * y, axis=-1, keepdims=True)
        ysc_out[rows, :] = (y * lax.rsqrt(ms + RMS_EPS) * scg_ref[...]).astype(ysc_out.dtype)
        return c

    lax.fori_loop(0, TS // RB, conv_body, 0)

    _gate_preacts(ub, wg_ref, za, zx)

    cpos = _c_softplus(lam_ref[...])

    def scan_body(k, carry):
        rows = pl.ds(pl.multiple_of(k * RB, RB), RB)
        a, b = _gate_terms(u_out[rows, :], za[rows, :] + ba_ref[...], zx[rows, :] + bx_ref[...], cpos)
        h, carry = _scan_rows(a, b, carry, reverse=False)
        hf_out[rows, :] = h
        return carry

    hc[...] = lax.fori_loop(0, TS // RB, scan_body, hc[...])


def _pass2_kernel(u_ref, g_ref, hf_ref, ysc_ref, x_ref, wo_ref, wg_ref, ba_ref, bx_ref, lam_ref,
                  lg_ref, lng_ref, lnb_ref,
                  o_ref,
                  ub, za, zx, ycat, mixed, hc):
    i = pl.program_id(0)

    @pl.when(i == 0)
    def _():
        hc[...] = jnp.zeros_like(hc)

    ub[...] = u_ref[...].astype(jnp.bfloat16)
    _gate_preacts(ub, wg_ref, za, zx)

    cpos = _c_softplus(lam_ref[...])
    nblk = TS // RB

    def scan_body(k, carry):
        rows = pl.ds(pl.multiple_of((nblk - 1 - k) * RB, RB), RB)
        a, b = _gate_terms(u_ref[rows, :], za[rows, :] + ba_ref[...], zx[rows, :] + bx_ref[...], cpos)
        hb, carry = _scan_rows(a, b, carry, reverse=True)
        y = (hf_ref[rows, :] + hb) * g_ref[rows, :]
        ms = jnp.mean(y * y, axis=-1, keepdims=True)
        ycat[rows, 0:WIDTH] = (y * lax.rsqrt(ms + RMS_EPS) * lg_ref[...]).astype(jnp.bfloat16)
        ycat[rows, WIDTH:] = ysc_ref[rows, :].astype(jnp.bfloat16)
        return carry

    hc[...] = lax.fori_loop(0, nblk, scan_body, hc[...])

    mixed[...] = jnp.dot(ycat[...], wo_ref[...], preferred_element_type=jnp.float32)

    def ln_body(k, c):
        rows = pl.ds(pl.multiple_of(k * RB, RB), RB)
        r = DEEPNORM_ALPHA * x_ref[rows, :] + mixed[rows, :]
        mu = jnp.mean(r, axis=-1, keepdims=True)
        d = r - mu
        var = jnp.mean(d * d, axis=-1, keepdims=True)
        o_ref[rows, :] = d * lax.rsqrt(var + LN_EPS) * lng_ref[...] + lnb_ref[...]
        return c

    lax.fori_loop(0, nblk, ln_body, 0)


def _block_diag_tiles(w):
    per = GATE_TILE // HEAD_DIM
    w4 = w.reshape(N_GATE_TILES, per, HEAD_DIM, HEAD_DIM)
    eye = jnp.eye(per, dtype=w.dtype)
    return jnp.einsum('qhij,hk->qhikj', w4, eye).reshape(N_GATE_TILES, GATE_TILE, GATE_TILE)


def _pack_gates(wa, wx):
    return jnp.concatenate([_block_diag_tiles(wa), _block_diag_tiles(wx)], axis=-1).astype(jnp.bfloat16)


def kernel(x, w_in, lru_conv_w, lru_conv_b, lru_wa_f, lru_ba_f, lru_wx_f, lru_bx_f, lru_lam_f,
           lru_wa_b, lru_ba_b, lru_wx_b, lru_bx_b, lru_lam_b, sc_conv_w, lru_norm_g, sc_norm_g,
           w_out, ln_g, ln_b):
    bsz, seq, d = x.shape
    assert bsz == 1 and d == D_MODEL and seq % TS == 0 and w_in.shape == (1, D_MODEL, PROJ_COLS)
    n_steps = seq // TS
    x2 = x.reshape(seq, d)
    row = lambda v: v.reshape(1, -1)

    w_in_b = w_in[0].astype(jnp.bfloat16)
    w_out_b = w_out[0].astype(jnp.bfloat16)
    wg_f = _pack_gates(lru_wa_f[0], lru_wx_f[0])
    wg_b = _pack_gates(lru_wa_b[0], lru_wx_b[0])

    chunk = lambda w: pl.BlockSpec((TS, w), lambda i: (i, 0))
    halo_blocks = seq // HALO
    halo_spec = pl.BlockSpec(
        (HALO, D_MODEL), lambda i: (jnp.minimum((i + 1) * (TS // HALO), halo_blocks - 1), 0))
    whole = pl.BlockSpec(memory_space=pltpu.VMEM)
    f32 = jnp.float32

    u, g, hf, ysc = pl.pallas_call(
        _pass1_kernel,
        grid=(n_steps,),
        in_specs=[chunk(D_MODEL), halo_spec] + [whole] * 9,
        out_specs=[chunk(WIDTH)] * 4,
        out_shape=[jax.ShapeDtypeStruct((seq, WIDTH), f32)] * 4,
        scratch_shapes=[
            pltpu.VMEM((TS + HALO, D_MODEL), jnp.bfloat16),
            pltpu.VMEM((LEAD + TS + HALO, PROJ_COLS), f32),
            pltpu.VMEM((LEAD + TS + HALO, WIDTH), f32),
            pltpu.VMEM((TS, WIDTH), jnp.bfloat16),
            pltpu.VMEM((TS, WIDTH), f32),
            pltpu.VMEM((TS, WIDTH), f32),
            pltpu.VMEM((SUBLANES, WIDTH), f32),
        ],
        compiler_params=pltpu.CompilerParams(
            dimension_semantics=("arbitrary",), vmem_limit_bytes=VMEM_LIMIT),
        name="lru_sc_forward_pass",
    )(x2, x2, w_in_b, lru_conv_w[0], row(lru_conv_b[0]), wg_f, row(lru_ba_f[0]), row(lru_bx_f[0]),
      row(lru_lam_f[0]), sc_conv_w[0], row(sc_norm_g[0]))

    rev = lambda w: pl.BlockSpec((TS, w), lambda i: (n_steps - 1 - i, 0))
    out = pl.pallas_call(
        _pass2_kernel,
        grid=(n_steps,),
        in_specs=[rev(WIDTH)] * 4 + [rev(D_MODEL)] + [whole] * 8,
        out_specs=rev(D_MODEL),
        out_shape=jax.ShapeDtypeStruct((seq, D_MODEL), x.dtype),
        scratch_shapes=[
            pltpu.VMEM((TS, WIDTH), jnp.bfloat16),
            pltpu.VMEM((TS, WIDTH), f32),
            pltpu.VMEM((TS, WIDTH), f32),
            pltpu.VMEM((TS, 2 * WIDTH), jnp.bfloat16),
            pltpu.VMEM((TS, D_MODEL), f32),
            pltpu.VMEM((SUBLANES, WIDTH), f32),
        ],
        compiler_params=pltpu.CompilerParams(
            dimension_semantics=("arbitrary",), vmem_limit_bytes=VMEM_LIMIT),
        name="lru_backward_out_pass",
    )(u, g, hf, ysc, x2, w_out_b, wg_b, row(lru_ba_b[0]), row(lru_bx_b[0]), row(lru_lam_b[0]),
      row(lru_norm_g[0]), row(ln_g[0]), row(ln_b[0]))

    return out.reshape(bsz, seq, d)
```

```python
import functools

import jax
import jax.numpy as jnp
from jax import lax
from jax.experimental import pallas as pl
from jax.experimental.pallas import tpu as pltpu

D_MODEL = 2048
WIDTH = 1024
HEADS = 16
HEAD_DIM = WIDTH // HEADS
PROJ_COLS = 6 * WIDTH
LRU_C = 8.0
DEEPNORM_ALPHA = 2.0 ** 0.25
LN_EPS = 1e-5
RMS_EPS = 1e-6

SUBLANES = 8
GATE_TILE = 256
N_GATE_TILES = WIDTH // GATE_TILE
TS = 256
HALO = 16
LEAD = SUBLANES
RB = 32
VMEM_LIMIT = 56 * 1024 * 1024

C_LX, C_LG, C_SB, C_SC, C_SH, C_SG = (k * WIDTH for k in range(6))


def _sigmoid(z):
    return 0.5 * jnp.tanh(0.5 * z) + 0.5


def _silu(z):
    return z * _sigmoid(z)


def _c_softplus(lam):
    sp = jnp.maximum(-lam, 0.0) + jnp.log1p(jnp.exp(-jnp.abs(lam)))
    return LRU_C * sp


def _gate_terms(u, za, zx, cpos):
    r = _sigmoid(za)
    i = _sigmoid(zx)
    neg_log_a = r * cpos
    a = jnp.exp(-neg_log_a)
    mult = jnp.sqrt(jnp.tanh(neg_log_a) * (1.0 + a * a))
    return a, mult * (i * u)


def _scan8(a, b, carry, reverse):
    row = lax.broadcasted_iota(jnp.int32, a.shape, 0)
    for s in (1, 2, 4):
        if reverse:
            shift, m = SUBLANES - s, row < SUBLANES - s
        else:
            shift, m = s, row >= s
        a_sh = jnp.where(m, pltpu.roll(a, shift, 0), 1.0)
        b_sh = jnp.where(m, pltpu.roll(b, shift, 0), 0.0)
        b = a * b_sh + b
        a = a * a_sh
    return a * carry + b


def _scan_rows(a, b, carry, reverse):
    groups = list(range(RB // SUBLANES))
    if reverse:
        groups = groups[::-1]
    out = [None] * len(groups)
    for gi in groups:
        sl = slice(gi * SUBLANES, (gi + 1) * SUBLANES)
        h = _scan8(a[sl], b[sl], carry, reverse)
        edge = h[0:1] if reverse else h[SUBLANES - 1:SUBLANES]
        carry = jnp.broadcast_to(edge, h.shape)
        out[gi] = h
    return jnp.concatenate(out, axis=0), carry


def _gate_preacts(ub_ref, wg_ref, za_ref, zx_ref):
    for q in range(N_GATE_TILES):
        cs = slice(q * GATE_TILE, (q + 1) * GATE_TILE)
        z = jnp.dot(ub_ref[:, cs], wg_ref[q], preferred_element_type=jnp.float32)
        za_ref[:, cs] = z[:, :GATE_TILE]
        zx_ref[:, cs] = z[:, GATE_TILE:]


def _pass1_kernel(x_ref, xn_ref, w_ref, cw_ref, cb_ref, wg_ref, ba_ref, bx_ref, lam_ref,
                  scw_ref, scg_ref,
                  u_out, g_out, hf_out, ysc_out,
                  xb, proj, ch, ub, za, zx, hc):
    i = pl.program_id(0)
    last = pl.num_programs(0) - 1

    @pl.when(i == 0)
    def _():
        proj[0:LEAD, C_LX:C_LX + WIDTH] = jnp.zeros((LEAD, WIDTH), jnp.float32)
        ch[0:LEAD, :] = jnp.zeros((LEAD, WIDTH), jnp.float32)
        hc[...] = jnp.zeros_like(hc)

    @pl.when(i > 0)
    def _():
        proj[0:LEAD, C_LX:C_LX + WIDTH] = proj[TS:TS + LEAD, C_LX:C_LX + WIDTH]
        ch[0:LEAD, :] = ch[TS:TS + LEAD, :]

    keep = jnp.where(i == last, 0.0, 1.0)
    xb[0:TS, :] = x_ref[...].astype(jnp.bfloat16)
    xb[TS:TS + HALO, :] = (xn_ref[...] * keep).astype(jnp.bfloat16)

    def project(c0, nrows):
        proj[LEAD:LEAD + nrows, c0:c0 + WIDTH] = jnp.dot(
            xb[0:nrows, :], w_ref[:, c0:c0 + WIDTH], preferred_element_type=jnp.float32)

    def tap(ref, r0, off, c0):
        s = LEAD + r0 + off
        return ref[s:s + RB, c0:c0 + WIDTH]

    blocks = [k * RB for k in range(TS // RB)]

    project(C_LX, TS + HALO)
    for r0 in blocks:
        u = (tap(proj, r0, -2, C_LX) * cw_ref[0:1, :] + tap(proj, r0, -1, C_LX) * cw_ref[1:2, :]
             + tap(proj, r0, 0, C_LX) * cw_ref[2:3, :] + tap(proj, r0, 1, C_LX) * cw_ref[3:4, :]
             + cb_ref[...])
        u_out[r0:r0 + RB, :] = u
        ub[r0:r0 + RB, :] = u.astype(jnp.bfloat16)

    _gate_preacts(ub, wg_ref, za, zx)
    project(C_SC, TS + HALO)
    project(C_SH, TS + HALO)

    cpos = _c_softplus(lam_ref[...])
    carry = hc[...]
    for r0 in blocks:
        rows = slice(r0, r0 + RB)
        a, b = _gate_terms(u_out[rows, :], za[rows, :] + ba_ref[...], zx[rows, :] + bx_ref[...], cpos)
        h, carry = _scan_rows(a, b, carry, reverse=False)
        hf_out[rows, :] = h
    hc[...] = carry

    project(C_SB, TS)
    project(C_SG, TS)
    for r0 in range(0, TS + HALO, RB // 2):
        rows = slice(LEAD + r0, LEAD + r0 + RB // 2)
        ch[rows, :] = proj[rows, C_SC:C_SC + WIDTH] * proj[rows, C_SH:C_SH + WIDTH]
    for r0 in blocks:
        prow = slice(LEAD + r0, LEAD + r0 + RB)
        v = (tap(ch, r0, -1, 0) * scw_ref[0:1, :] + tap(ch, r0, 0, 0) * scw_ref[1:2, :]
             + tap(ch, r0, 1, 0) * scw_ref[2:3, :])
        y = proj[prow, C_SB:C_SB + WIDTH] * v * _silu(proj[prow, C_SG:C_SG + WIDTH])
        ms = jnp.mean(y * y, axis=-1, keepdims=True)
        ysc_out[r0:r0 + RB, :] = (y * lax.rsqrt(ms + RMS_EPS) * scg_ref[...]).astype(ysc_out.dtype)

    project(C_LG, TS)
    for r0 in blocks:
        g_out[r0:r0 + RB, :] = _silu(proj[LEAD + r0:LEAD + r0 + RB, C_LG:C_LG + WIDTH])


def _pass2_kernel(u_ref, g_ref, hf_ref, ysc_ref, x_ref, wo_ref, wg_ref, ba_ref, bx_ref, lam_ref,
                  lg_ref, lng_ref, lnb_ref,
                  o_ref,
                  ub, za, zx, ycat, mixed, hc):
    i = pl.program_id(0)

    @pl.when(i == 0)
    def _():
        hc[...] = jnp.zeros_like(hc)

    ub[...] = u_ref[...].astype(jnp.bfloat16)
    _gate_preacts(ub, wg_ref, za, zx)

    cpos = _c_softplus(lam_ref[...])
    nblk = TS // RB

    def scan_body(k, carry):
        rows = pl.ds(pl.multiple_of((nblk - 1 - k) * RB, RB), RB)
        a, b = _gate_terms(u_ref[rows, :], za[rows, :] + ba_ref[...], zx[rows, :] + bx_ref[...], cpos)
        hb, carry = _scan_rows(a, b, carry, reverse=True)
        y = (hf_ref[rows, :] + hb) * g_ref[rows, :]
        ms = jnp.mean(y * y, axis=-1, keepdims=True)
        ycat[rows, 0:WIDTH] = (y * lax.rsqrt(ms + RMS_EPS) * lg_ref[...]).astype(jnp.bfloat16)
        ycat[rows, WIDTH:] = ysc_ref[rows, :].astype(jnp.bfloat16)
        return carry

    hc[...] = lax.fori_loop(0, nblk, scan_body, hc[...])

    mixed[...] = jnp.dot(ycat[...], wo_ref[...], preferred_element_type=jnp.float32)

    def ln_body(k, c):
        rows = pl.ds(pl.multiple_of(k * RB, RB), RB)
        r = DEEPNORM_ALPHA * x_ref[rows, :] + mixed[rows, :]
        mu = jnp.mean(r, axis=-1, keepdims=True)
        d = r - mu
        var = jnp.mean(d * d, axis=-1, keepdims=True)
        o_ref[rows, :] = d * lax.rsqrt(var + LN_EPS) * lng_ref[...] + lnb_ref[...]
        return c

    lax.fori_loop(0, nblk, ln_body, 0)


def _block_diag_tiles(w):
    per = GATE_TILE // HEAD_DIM
    w4 = w.reshape(N_GATE_TILES, per, HEAD_DIM, HEAD_DIM)
    eye = jnp.eye(per, dtype=w.dtype)
    return jnp.einsum('qhij,hk->qhikj', w4, eye).reshape(N_GATE_TILES, GATE_TILE, GATE_TILE)


def _pack_gates(wa, wx):
    return jnp.concatenate([_block_diag_tiles(wa), _block_diag_tiles(wx)], axis=-1).astype(jnp.bfloat16)


def kernel(x, w_in, lru_conv_w, lru_conv_b, lru_wa_f, lru_ba_f, lru_wx_f, lru_bx_f, lru_lam_f,
           lru_wa_b, lru_ba_b, lru_wx_b, lru_bx_b, lru_lam_b, sc_conv_w, lru_norm_g, sc_norm_g,
           w_out, ln_g, ln_b):
    bsz, seq, d = x.shape
    assert bsz == 1 and d == D_MODEL and seq % TS == 0 and w_in.shape == (1, D_MODEL, PROJ_COLS)
    n_steps = seq // TS
    x2 = x.reshape(seq, d)
    row = lambda v: v.reshape(1, -1)

    w_in_b = w_in[0].astype(jnp.bfloat16)
    w_out_b = w_out[0].astype(jnp.bfloat16)
    wg_f = _pack_gates(lru_wa_f[0], lru_wx_f[0])
    wg_b = _pack_gates(lru_wa_b[0], lru_wx_b[0])

    chunk = lambda w: pl.BlockSpec((TS, w), lambda i: (i, 0))
    halo_blocks = seq // HALO
    halo_spec = pl.BlockSpec(
        (HALO, D_MODEL), lambda i: (jnp.minimum((i + 1) * (TS // HALO), halo_blocks - 1), 0))
    whole = pl.BlockSpec(memory_space=pltpu.VMEM)
    f32 = jnp.float32

    u, g, hf, ysc = pl.pallas_call(
        _pass1_kernel,
        grid=(n_steps,),
        in_specs=[chunk(D_MODEL), halo_spec] + [whole] * 9,
        out_specs=[chunk(WIDTH)] * 4,
        out_shape=[jax.ShapeDtypeStruct((seq, WIDTH), f32)] * 4,
        scratch_shapes=[
            pltpu.VMEM((TS + HALO, D_MODEL), jnp.bfloat16),
            pltpu.VMEM((LEAD + TS + HALO, PROJ_COLS), f32),
            pltpu.VMEM((LEAD + TS + HALO, WIDTH), f32),
            pltpu.VMEM((TS, WIDTH), jnp.bfloat16),
            pltpu.VMEM((TS, WIDTH), f32),
            pltpu.VMEM((TS, WIDTH), f32),
            pltpu.VMEM((SUBLANES, WIDTH), f32),
        ],
        compiler_params=pltpu.CompilerParams(
            dimension_semantics=("arbitrary",), vmem_limit_bytes=VMEM_LIMIT),
        name="lru_sc_forward_pass",
    )(x2, x2, w_in_b, lru_conv_w[0], row(lru_conv_b[0]), wg_f, row(lru_ba_f[0]), row(lru_bx_f[0]),
      row(lru_lam_f[0]), sc_conv_w[0], row(sc_norm_g[0]))

    rev = lambda w: pl.BlockSpec((TS, w), lambda i: (n_steps - 1 - i, 0))
    out = pl.pallas_call(
        _pass2_kernel,
        grid=(n_steps,),
        in_specs=[rev(WIDTH)] * 4 + [rev(D_MODEL)] + [whole] * 8,
        out_specs=rev(D_MODEL),
        out_shape=jax.ShapeDtypeStruct((seq, D_MODEL), x.dtype),
        scratch_shapes=[
            pltpu.VMEM((TS, WIDTH), jnp.bfloat16),
            pltpu.VMEM((TS, WIDTH), f32),
            pltpu.VMEM((TS, WIDTH), f32),
            pltpu.VMEM((TS, 2 * WIDTH), jnp.bfloat16),
            pltpu.VMEM((TS, D_MODEL), f32),
            pltpu.VMEM((SUBLANES, WIDTH), f32),
        ],
        compiler_params=pltpu.CompilerParams(
            dimension_semantics=("arbitrary",), vmem_limit_bytes=VMEM_LIMIT),
        name="lru_backward_out_pass",
    )(u, g, hf, ysc, x2, w_out_b, wg_b, row(lru_ba_b[0]), row(lru_bx_b[0]), row(lru_lam_b[0]),
      row(lru_norm_g[0]), row(ln_g[0]), row(ln_b[0]))

    return out.reshape(bsz, seq, d)
```

```python
import functools

import jax
import jax.numpy as jnp
from jax import lax
from jax.experimental import pallas as pl
from jax.experimental.pallas import tpu as pltpu

D_MODEL = 2048
WIDTH = 1024
HEADS = 16
HEAD_DIM = WIDTH // HEADS
PROJ_COLS = 6 * WIDTH
LRU_C = 8.0
DEEPNORM_ALPHA = 2.0 ** 0.25
LN_EPS = 1e-5
RMS_EPS = 1e-6

SUBLANES = 8
GATE_TILE = 256
N_GATE_TILES = WIDTH // GATE_TILE
TS = 256
HALO = 16
LEAD = SUBLANES
RB = 32
SUB = 128
VMEM_LIMIT = 56 * 1024 * 1024

C_LX, C_LG, C_SB, C_SC, C_SH, C_SG = (k * WIDTH for k in range(6))


def _sigmoid(z):
    return 0.5 * jnp.tanh(0.5 * z) + 0.5


def _silu(z):
    return z * _sigmoid(z)


def _c_softplus(lam):
    sp = jnp.maximum(-lam, 0.0) + jnp.log1p(jnp.exp(-jnp.abs(lam)))
    return LRU_C * sp


def _gate_terms(u, za, zx, cpos):
    r = _sigmoid(za)
    i = _sigmoid(zx)
    neg_log_a = r * cpos
    a = jnp.exp(-neg_log_a)
    mult = jnp.sqrt(jnp.tanh(neg_log_a) * (1.0 + a * a))
    return a, mult * (i * u)


def _scan8(a, b, carry, reverse):
    row = lax.broadcasted_iota(jnp.int32, a.shape, 0)
    for s in (1, 2, 4):
        if reverse:
            shift, m = SUBLANES - s, row < SUBLANES - s
        else:
            shift, m = s, row >= s
        a_sh = jnp.where(m, pltpu.roll(a, shift, 0), 1.0)
        b_sh = jnp.where(m, pltpu.roll(b, shift, 0), 0.0)
        b = a * b_sh + b
        a = a * a_sh
    return a * carry + b


def _scan_rows(a, b, carry, reverse):
    groups = list(range(RB // SUBLANES))
    if reverse:
        groups = groups[::-1]
    out = [None] * len(groups)
    for gi in groups:
        sl = slice(gi * SUBLANES, (gi + 1) * SUBLANES)
        h = _scan8(a[sl], b[sl], carry, reverse)
        edge = h[0:1] if reverse else h[SUBLANES - 1:SUBLANES]
        carry = jnp.broadcast_to(edge, h.shape)
        out[gi] = h
    return jnp.concatenate(out, axis=0), carry


def _gate_preacts(ub_ref, wg_ref, za_ref, zx_ref, rows=slice(None)):
    for q in range(N_GATE_TILES):
        cs = slice(q * GATE_TILE, (q + 1) * GATE_TILE)
        z = jnp.dot(ub_ref[rows, cs], wg_ref[q], preferred_element_type=jnp.float32)
        za_ref[rows, cs] = z[:, :GATE_TILE]
        zx_ref[rows, cs] = z[:, GATE_TILE:]


def _pass1_kernel(x_ref, xn_ref, w_ref, cw_ref, cb_ref, wg_ref, ba_ref, bx_ref, lam_ref,
                  scw_ref, scg_ref,
                  u_out, g_out, hf_out, ysc_out,
                  xb, proj, ch, ub, za, zx, hc):
    i = pl.program_id(0)
    last = pl.num_programs(0) - 1

    @pl.when(i == 0)
    def _():
        proj[0:LEAD, C_LX:C_LX + WIDTH] = jnp.zeros((LEAD, WIDTH), jnp.float32)
        ch[0:LEAD, :] = jnp.zeros((LEAD, WIDTH), jnp.float32)
        hc[...] = jnp.zeros_like(hc)

    @pl.when(i > 0)
    def _():
        proj[0:LEAD, C_LX:C_LX + WIDTH] = proj[TS:TS + LEAD, C_LX:C_LX + WIDTH]
        ch[0:LEAD, :] = ch[TS:TS + LEAD, :]

    keep = jnp.where(i == last, 0.0, 1.0)
    xb[0:TS, :] = x_ref[...].astype(jnp.bfloat16)
    xb[TS:TS + HALO, :] = (xn_ref[...] * keep).astype(jnp.bfloat16)

    def project(c0, nrows):
        proj[LEAD:LEAD + nrows, c0:c0 + WIDTH] = jnp.dot(
            xb[0:nrows, :], w_ref[:, c0:c0 + WIDTH], preferred_element_type=jnp.float32)

    def tap(ref, r0, off, c0):
        s = LEAD + r0 + off
        return ref[s:s + RB, c0:c0 + WIDTH]

    blocks = [k * RB for k in range(TS // RB)]

    project(C_LX, TS + HALO)
    for r0 in blocks:
        u = (tap(proj, r0, -2, C_LX) * cw_ref[0:1, :] + tap(proj, r0, -1, C_LX) * cw_ref[1:2, :]
             + tap(proj, r0, 0, C_LX) * cw_ref[2:3, :] + tap(proj, r0, 1, C_LX) * cw_ref[3:4, :]
             + cb_ref[...])
        u_out[r0:r0 + RB, :] = u
        ub[r0:r0 + RB, :] = u.astype(jnp.bfloat16)

    _gate_preacts(ub, wg_ref, za, zx)
    project(C_SC, TS + HALO)
    project(C_SH, TS + HALO)

    cpos = _c_softplus(lam_ref[...])
    carry = hc[...]
    for r0 in blocks:
        rows = slice(r0, r0 + RB)
        a, b = _gate_terms(u_out[rows, :], za[rows, :] + ba_ref[...], zx[rows, :] + bx_ref[...], cpos)
        h, carry = _scan_rows(a, b, carry, reverse=False)
        hf_out[rows, :] = h
    hc[...] = carry

    project(C_SB, TS)
    project(C_SG, TS)
    for r0 in range(0, TS + HALO, RB // 2):
        rows = slice(LEAD + r0, LEAD + r0 + RB // 2)
        ch[rows, :] = proj[rows, C_SC:C_SC + WIDTH] * proj[rows, C_SH:C_SH + WIDTH]
    for r0 in blocks:
        prow = slice(LEAD + r0, LEAD + r0 + RB)
        v = (tap(ch, r0, -1, 0) * scw_ref[0:1, :] + tap(ch, r0, 0, 0) * scw_ref[1:2, :]
             + tap(ch, r0, 1, 0) * scw_ref[2:3, :])
        y = proj[prow, C_SB:C_SB + WIDTH] * v * _silu(proj[prow, C_SG:C_SG + WIDTH])
        ms = jnp.mean(y * y, axis=-1, keepdims=True)
        ysc_out[r0:r0 + RB, :] = (y * lax.rsqrt(ms + RMS_EPS) * scg_ref[...]).astype(ysc_out.dtype)

    project(C_LG, TS)
    for r0 in blocks:
        g_out[r0:r0 + RB, :] = _silu(proj[LEAD + r0:LEAD + r0 + RB, C_LG:C_LG + WIDTH])


def _pass2_kernel(u_ref, g_ref, hf_ref, ysc_ref, x_ref, wo_ref, wg_ref, ba_ref, bx_ref, lam_ref,
                  lg_ref, lng_ref, lnb_ref,
                  o_ref,
                  ub, za, zx, ycat, mixed, hc):
    i = pl.program_id(0)

    @pl.when(i == 0)
    def _():
        hc[...] = jnp.zeros_like(hc)

    cpos = _c_softplus(lam_ref[...])
    carry = hc[...]

    for s0 in range(TS - SUB, -1, -SUB):
        sub = slice(s0, s0 + SUB)
        ub[sub, :] = u_ref[sub, :].astype(jnp.bfloat16)
        _gate_preacts(ub, wg_ref, za, zx, sub)
        for r0 in range(s0 + SUB - RB, s0 - 1, -RB):
            rows = slice(r0, r0 + RB)
            a, b = _gate_terms(u_ref[rows, :], za[rows, :] + ba_ref[...], zx[rows, :] + bx_ref[...], cpos)
            hb, carry = _scan_rows(a, b, carry, reverse=True)
            y = (hf_ref[rows, :] + hb) * g_ref[rows, :]
            ms = jnp.mean(y * y, axis=-1, keepdims=True)
            ycat[rows, 0:WIDTH] = (y * lax.rsqrt(ms + RMS_EPS) * lg_ref[...]).astype(jnp.bfloat16)
            ycat[rows, WIDTH:] = ysc_ref[rows, :]
        mixed[sub, :] = jnp.dot(ycat[sub, :], wo_ref[...], preferred_element_type=jnp.float32)
        for r0 in range(s0 + SUB - RB, s0 - 1, -RB):
            rows = slice(r0, r0 + RB)
            r = DEEPNORM_ALPHA * x_ref[rows, :] + mixed[rows, :]
            mu = jnp.mean(r, axis=-1, keepdims=True)
            d = r - mu
            var = jnp.mean(d * d, axis=-1, keepdims=True)
            o_ref[rows, :] = d * lax.rsqrt(var + LN_EPS) * lng_ref[...] + lnb_ref[...]
    hc[...] = carry


def _block_diag_tiles(w):
    per = GATE_TILE // HEAD_DIM
    w4 = w.reshape(N_GATE_TILES, per, HEAD_DIM, HEAD_DIM)
    eye = jnp.eye(per, dtype=w.dtype)
    return jnp.einsum('qhij,hk->qhikj', w4, eye).reshape(N_GATE_TILES, GATE_TILE, GATE_TILE)


def _pack_gates(wa, wx):
    return jnp.concatenate([_block_diag_tiles(wa), _block_diag_tiles(wx)], axis=-1).astype(jnp.bfloat16)


def kernel(x, w_in, lru_conv_w, lru_conv_b, lru_wa_f, lru_ba_f, lru_wx_f, lru_bx_f, lru_lam_f,
           lru_wa_b, lru_ba_b, lru_wx_b, lru_bx_b, lru_lam_b, sc_conv_w, lru_norm_g, sc_norm_g,
           w_out, ln_g, ln_b):
    bsz, seq, d = x.shape
    assert bsz == 1 and d == D_MODEL and seq % TS == 0 and w_in.shape == (1, D_MODEL, PROJ_COLS)
    n_steps = seq // TS
    x2 = x.reshape(seq, d)
    row = lambda v: v.reshape(1, -1)

    w_in_b = w_in[0].astype(jnp.bfloat16)
    w_out_b = w_out[0].astype(jnp.bfloat16)
    wg_f = _pack_gates(lru_wa_f[0], lru_wx_f[0])
    wg_b = _pack_gates(lru_wa_b[0], lru_wx_b[0])

    chunk = lambda w: pl.BlockSpec((TS, w), lambda i: (i, 0))
    halo_blocks = seq // HALO
    halo_spec = pl.BlockSpec(
        (HALO, D_MODEL), lambda i: (jnp.minimum((i + 1) * (TS // HALO), halo_blocks - 1), 0))
    whole = pl.BlockSpec(memory_space=pltpu.VMEM)
    f32 = jnp.float32

    u, g, hf, ysc = pl.pallas_call(
        _pass1_kernel,
        grid=(n_steps,),
        in_specs=[chunk(D_MODEL), halo_spec] + [whole] * 9,
        out_specs=[chunk(WIDTH)] * 4,
        out_shape=[jax.ShapeDtypeStruct((seq, WIDTH), f32)] * 3
        + [jax.ShapeDtypeStruct((seq, WIDTH), jnp.bfloat16)],
        scratch_shapes=[
            pltpu.VMEM((TS + HALO, D_MODEL), jnp.bfloat16),
            pltpu.VMEM((LEAD + TS + HALO, PROJ_COLS), f32),
            pltpu.VMEM((LEAD + TS + HALO, WIDTH), f32),
            pltpu.VMEM((TS, WIDTH), jnp.bfloat16),
            pltpu.VMEM((TS, WIDTH), f32),
            pltpu.VMEM((TS, WIDTH), f32),
            pltpu.VMEM((SUBLANES, WIDTH), f32),
        ],
        compiler_params=pltpu.CompilerParams(
            dimension_semantics=("arbitrary",), vmem_limit_bytes=VMEM_LIMIT),
        name="lru_sc_forward_pass",
    )(x2, x2, w_in_b, lru_conv_w[0], row(lru_conv_b[0]), wg_f, row(lru_ba_f[0]), row(lru_bx_f[0]),
      row(lru_lam_f[0]), sc_conv_w[0], row(sc_norm_g[0]))

    rev = lambda w: pl.BlockSpec((TS, w), lambda i: (n_steps - 1 - i, 0))
    out = pl.pallas_call(
        _pass2_kernel,
        grid=(n_steps,),
        in_specs=[rev(WIDTH)] * 4 + [rev(D_MODEL)] + [whole] * 8,
        out_specs=rev(D_MODEL),
        out_shape=jax.ShapeDtypeStruct((seq, D_MODEL), x.dtype),
        scratch_shapes=[
            pltpu.VMEM((TS, WIDTH), jnp.bfloat16),
            pltpu.VMEM((TS, WIDTH), f32),
            pltpu.VMEM((TS, WIDTH), f32),
            pltpu.VMEM((TS, 2 * WIDTH), jnp.bfloat16),
            pltpu.VMEM((TS, D_MODEL), f32),
            pltpu.VMEM((SUBLANES, WIDTH), f32),
        ],
        compiler_params=pltpu.CompilerParams(
            dimension_semantics=("arbitrary",), vmem_limit_bytes=VMEM_LIMIT),
        name="lru_backward_out_pass",
    )(u, g, hf, ysc, x2, w_out_b, wg_b, row(lru_ba_b[0]), row(lru_bx_b[0]), row(lru_lam_b[0]),
      row(lru_norm_g[0]), row(ln_g[0]), row(ln_b[0]))

    return out.reshape(bsz, seq, d)
```

```python
import functools

import jax
import jax.numpy as jnp
from jax import lax
from jax.experimental import pallas as pl
from jax.experimental.pallas import tpu as pltpu

D_MODEL = 2048
WIDTH = 1024
HEADS = 16
HEAD_DIM = WIDTH // HEADS
PROJ_COLS = 6 * WIDTH
LRU_C = 8.0
DEEPNORM_ALPHA = 2.0 ** 0.25
LN_EPS = 1e-5
RMS_EPS = 1e-6
SQRT_FLOOR = 1e-36

SUBLANES = 8
GATE_TILE = 256
N_GATE_TILES = WIDTH // GATE_TILE
TS = 256
HALO = 16
LEAD = SUBLANES
RB = 32
PIPE_LAG = 2
N_TRIPS = 4
TRIP_ROWS = TS // N_TRIPS
SCAN_RB = 16
VMEM_LIMIT = 56 * 1024 * 1024

C_LX, C_LG, C_SB, C_SC, C_SH, C_SG = (k * WIDTH for k in range(6))


def _sigmoid(z):
    return 0.5 * jnp.tanh(0.5 * z) + 0.5


def _pack_row_pairs(w):
    *lead, k, n = w.shape
    pairs = jnp.swapaxes(w.reshape(*lead, k // 2, 2, n), -1, -2)
    return lax.bitcast_convert_type(pairs, jnp.uint32)


def _as_bf16(packed):
    return pltpu.bitcast(packed, jnp.bfloat16)


def _silu(z):
    return z * _sigmoid(z)


def _half_log_decay(lam):
    sp = jnp.maximum(-lam, 0.0) + jnp.log1p(jnp.exp(-jnp.abs(lam)))
    return (-0.5 * LRU_C) * sp


def _gate_terms(u, za_half, zx_half, hld):
    t_a = jnp.tanh(za_half)
    t_x = jnp.tanh(zx_half)
    log_a = t_a * hld + hld
    a = jnp.exp(log_a)
    m2 = jnp.tanh(log_a) * (-1.0 - a * a)
    mult = m2 * lax.rsqrt(jnp.maximum(m2, SQRT_FLOOR))
    return a, (mult * (0.5 * t_x + 0.5)) * u


def _scan8(a, b, carry, reverse):
    row = lax.broadcasted_iota(jnp.int32, a.shape, 0)
    for s in (1, 2, 4):
        if reverse:
            shift, m = SUBLANES - s, row < SUBLANES - s
        else:
            shift, m = s, row >= s
        a_sh = jnp.where(m, pltpu.roll(a, shift, 0), 1.0)
        b_sh = jnp.where(m, pltpu.roll(b, shift, 0), 0.0)
        b = a * b_sh + b
        a = a * a_sh
    return a * carry + b


def _scan_rows(a, b, carry, reverse):
    groups = list(range(a.shape[0] // SUBLANES))
    if reverse:
        groups = groups[::-1]
    out = [None] * len(groups)
    for gi in groups:
        sl = slice(gi * SUBLANES, (gi + 1) * SUBLANES)
        h = _scan8(a[sl], b[sl], carry, reverse)
        edge = h[0:1] if reverse else h[SUBLANES - 1:SUBLANES]
        carry = jnp.broadcast_to(edge, h.shape)
        out[gi] = h
    return jnp.concatenate(out, axis=0), carry


def _gate_preacts(ub_ref, wg_ref, za_ref, zx_ref, rows=slice(None)):
    for q in range(N_GATE_TILES):
        cs = slice(q * GATE_TILE, (q + 1) * GATE_TILE)
        z = jnp.dot(ub_ref[rows, cs], _as_bf16(wg_ref[q]), preferred_element_type=jnp.float32)
        za_ref[rows, cs] = z[:, :GATE_TILE]
        zx_ref[rows, cs] = z[:, GATE_TILE:]


def _pass1_kernel(x_ref, xn_ref, w_ref, cw_ref, cb_ref, wg_ref, ba_ref, bx_ref, lam_ref,
                  scw_ref, scg_ref,
                  u_out, g_out, hf_out, ysc_out,
                  xb, plx, pc, ph, pb, psg, plg, ch, ub, za, zx, hc):
    i = pl.program_id(0)
    last = pl.num_programs(0) - 1

    @pl.when(i == 0)
    def _():
        plx[0:LEAD, :] = jnp.zeros((LEAD, WIDTH), jnp.float32)
        ch[0:LEAD, :] = jnp.zeros((LEAD, WIDTH), jnp.float32)
        hc[...] = jnp.zeros_like(hc)

    @pl.when(i > 0)
    def _():
        plx[0:LEAD, :] = plx[TS:TS + LEAD, :]
        ch[0:LEAD, :] = ch[TS:TS + LEAD, :]

    keep = jnp.where(i == last, 0.0, 1.0)
    xb[0:TS, :] = x_ref[...].astype(jnp.bfloat16)
    xb[TS:TS + HALO, :] = (xn_ref[...] * keep).astype(jnp.bfloat16)

    def project(dst, lead, c0, nrows):
        dst[lead:lead + nrows, :] = jnp.dot(
            xb[0:nrows, :], _as_bf16(w_ref[:, c0:c0 + WIDTH]), preferred_element_type=jnp.float32)

    def tap(ref, r0, off):
        s = LEAD + r0 + off
        return ref[s:s + RB, :]

    blocks = [k * RB for k in range(TS // RB)]

    project(plx, LEAD, C_LX, TS + HALO)
    project(pc, 0, C_SC, TS + HALO)
    project(ph, 0, C_SH, TS + HALO)
    for r0 in blocks:
        u = (tap(plx, r0, -2) * cw_ref[0:1, :] + tap(plx, r0, -1) * cw_ref[1:2, :]
             + tap(plx, r0, 0) * cw_ref[2:3, :] + tap(plx, r0, 1) * cw_ref[3:4, :]
             + cb_ref[...])
        u_out[r0:r0 + RB, :] = u
        ub[r0:r0 + RB, :] = u.astype(jnp.bfloat16)

    _gate_preacts(ub, wg_ref, za, zx)
    project(pb, 0, C_SB, TS)
    project(psg, 0, C_SG, TS)
    project(plg, 0, C_LG, TS)

    hld = _half_log_decay(lam_ref[...])
    carry = hc[...]
    for r0 in blocks:
        rows = slice(r0, r0 + RB)
        a, b = _gate_terms(u_out[rows, :], za[rows, :] + ba_ref[...], zx[rows, :] + bx_ref[...], hld)
        h, carry = _scan_rows(a, b, carry, reverse=False)
        hf_out[rows, :] = h
    hc[...] = carry

    for r0 in range(0, TS + HALO, RB // 2):
        rows = slice(r0, r0 + RB // 2)
        ch[LEAD + r0:LEAD + r0 + RB // 2, :] = pc[rows, :] * ph[rows, :]
    for r0 in blocks:
        rows = slice(r0, r0 + RB)
        v = (tap(ch, r0, -1) * scw_ref[0:1, :] + tap(ch, r0, 0) * scw_ref[1:2, :]
             + tap(ch, r0, 1) * scw_ref[2:3, :])
        y = pb[rows, :] * v * _silu(psg[rows, :])
        ms = jnp.mean(y * y, axis=-1, keepdims=True)
        ysc_out[rows, :] = (y * lax.rsqrt(ms + RMS_EPS) * scg_ref[...]).astype(ysc_out.dtype)

    for r0 in blocks:
        g_out[r0:r0 + RB, :] = _silu(plg[r0:r0 + RB, :])


def _pass2_kernel(u_ref, g_ref, hf_ref, ysc_ref, x_ref, wo_ref, wg_ref, ba_ref, bx_ref, lam_ref,
                  lg_ref, lng_ref, lnb_ref,
                  o_ref,
                  ub, za, zx, ycat_a, ycat_b, mixed_a, mixed_b, hc):
    i = pl.program_id(0)

    @pl.when(i == 0)
    def _():
        hc[...] = jnp.zeros_like(hc)
        ycat_b[...] = jnp.zeros_like(ycat_b)
        mixed_a[...] = jnp.zeros_like(mixed_a)


    def step(ycat_cur, ycat_prv, mixed_cur, mixed_prv):
        ub[...] = u_ref[...].astype(jnp.bfloat16)
        _gate_preacts(ub, wg_ref, za, zx)

        for r0 in range(0, TS, RB):
            rows = slice(r0, r0 + RB)
            m = jnp.concatenate([mixed_cur[k, rows, :] for k in range(N_TRIPS)], axis=-1)
            r = DEEPNORM_ALPHA * x_ref[rows, :] + m
            mu = jnp.mean(r, axis=-1, keepdims=True)
            d = r - mu
            var = jnp.mean(d * d, axis=-1, keepdims=True)
            o_ref[rows, :] = d * lax.rsqrt(var + LN_EPS) * lng_ref[...] + lnb_ref[...]

        hld = _half_log_decay(lam_ref[...])

        def body(k, carry):
            base = pl.multiple_of((N_TRIPS - 1 - k) * TRIP_ROWS, TRIP_ROWS)
            for off in range(TRIP_ROWS - SCAN_RB, -1, -SCAN_RB):
                rows = pl.ds(pl.multiple_of(base + off, SCAN_RB), SCAN_RB)
                a, b = _gate_terms(u_ref[rows, :], za[rows, :] + ba_ref[...], zx[rows, :] + bx_ref[...], hld)
                hb, carry = _scan_rows(a, b, carry, reverse=True)
                y = (hf_ref[rows, :] + hb) * g_ref[rows, :]
                ms = jnp.mean(y * y, axis=-1, keepdims=True)
                ycat_cur[rows, 0:WIDTH] = (y * lax.rsqrt(ms + RMS_EPS) * lg_ref[...]).astype(jnp.bfloat16)
                ycat_cur[rows, WIDTH:] = ysc_ref[rows, :]
            mixed_prv[k] = jnp.dot(ycat_prv[...], _as_bf16(wo_ref[k]),
                                   preferred_element_type=jnp.float32)
            return carry

        hc[...] = lax.fori_loop(0, N_TRIPS, body, hc[...])

    pl.when(i % 2 == 0)(functools.partial(step, ycat_a, ycat_b, mixed_a, mixed_b))
    pl.when(i % 2 == 1)(functools.partial(step, ycat_b, ycat_a, mixed_b, mixed_a))


def _block_diag_tiles(w):
    per = GATE_TILE // HEAD_DIM
    w4 = w.reshape(N_GATE_TILES, per, HEAD_DIM, HEAD_DIM)
    eye = jnp.eye(per, dtype=w.dtype)
    return jnp.einsum('qhij,hk->qhikj', w4, eye).reshape(N_GATE_TILES, GATE_TILE, GATE_TILE)


def _pack_gates(wa, wx):
    tiles = 0.5 * jnp.concatenate([_block_diag_tiles(wa), _block_diag_tiles(wx)], axis=-1)
    return _pack_row_pairs(tiles.astype(jnp.bfloat16))


def kernel(x, w_in, lru_conv_w, lru_conv_b, lru_wa_f, lru_ba_f, lru_wx_f, lru_bx_f, lru_lam_f,
           lru_wa_b, lru_ba_b, lru_wx_b, lru_bx_b, lru_lam_b, sc_conv_w, lru_norm_g, sc_norm_g,
           w_out, ln_g, ln_b):
    bsz, seq, d = x.shape
    assert bsz == 1 and d == D_MODEL and seq % TS == 0 and w_in.shape == (1, D_MODEL, PROJ_COLS)
    n_steps = seq // TS
    x2 = x.reshape(seq, d)
    row = lambda v: v.reshape(1, -1)

    half_row = lambda v: 0.5 * v.reshape(1, -1)
    w_in_b = _pack_row_pairs(w_in[0].astype(jnp.bfloat16))
    n_slices = N_TRIPS
    w_out_b = _pack_row_pairs(
        w_out[0].astype(jnp.bfloat16).reshape(D_MODEL, n_slices, D_MODEL // n_slices).transpose(1, 0, 2))
    wg_f = _pack_gates(lru_wa_f[0], lru_wx_f[0])
    wg_b = _pack_gates(lru_wa_b[0], lru_wx_b[0])

    chunk = lambda w: pl.BlockSpec((TS, w), lambda i: (i, 0))
    halo_blocks = seq // HALO
    halo_spec = pl.BlockSpec(
        (HALO, D_MODEL), lambda i: (jnp.minimum((i + 1) * (TS // HALO), halo_blocks - 1), 0))
    whole = pl.BlockSpec(memory_space=pltpu.VMEM)
    f32 = jnp.float32

    u, g, hf, ysc = pl.pallas_call(
        _pass1_kernel,
        grid=(n_steps,),
        in_specs=[chunk(D_MODEL), halo_spec] + [whole] * 9,
        out_specs=[chunk(WIDTH)] * 4,
        out_shape=[jax.ShapeDtypeStruct((seq, WIDTH), f32)] * 3
        + [jax.ShapeDtypeStruct((seq, WIDTH), jnp.bfloat16)],
        scratch_shapes=[
            pltpu.VMEM((TS + HALO, D_MODEL), jnp.bfloat16),
            pltpu.VMEM((LEAD + TS + HALO, WIDTH), f32),
            pltpu.VMEM((TS + HALO, WIDTH), f32),
            pltpu.VMEM((TS + HALO, WIDTH), f32),
            pltpu.VMEM((TS, WIDTH), f32),
            pltpu.VMEM((TS, WIDTH), f32),
            pltpu.VMEM((TS, WIDTH), f32),
            pltpu.VMEM((LEAD + TS + HALO, WIDTH), f32),
            pltpu.VMEM((TS, WIDTH), jnp.bfloat16),
            pltpu.VMEM((TS, WIDTH), f32),
            pltpu.VMEM((TS, WIDTH), f32),
            pltpu.VMEM((SUBLANES, WIDTH), f32),
        ],
        compiler_params=pltpu.CompilerParams(
            dimension_semantics=("arbitrary",), vmem_limit_bytes=VMEM_LIMIT),
        name="lru_sc_forward_pass",
    )(x2, x2, w_in_b, lru_conv_w[0], row(lru_conv_b[0]), wg_f, half_row(lru_ba_f[0]), half_row(lru_bx_f[0]),
      row(lru_lam_f[0]), sc_conv_w[0], row(sc_norm_g[0]))

    scan_chunk = lambda w: pl.BlockSpec((TS, w), lambda i: (jnp.maximum(n_steps - 1 - i, 0), 0))
    done_chunk = pl.BlockSpec(
        (TS, D_MODEL), lambda i: (jnp.minimum(n_steps - 1 + PIPE_LAG - i, n_steps - 1), 0))
    out = pl.pallas_call(
        _pass2_kernel,
        grid=(n_steps + PIPE_LAG,),
        in_specs=[scan_chunk(WIDTH)] * 4 + [done_chunk] + [whole] * 8,
        out_specs=done_chunk,
        out_shape=jax.ShapeDtypeStruct((seq, D_MODEL), x.dtype),
        scratch_shapes=[
            pltpu.VMEM((TS, WIDTH), jnp.bfloat16),
            pltpu.VMEM((TS, WIDTH), f32),
            pltpu.VMEM((TS, WIDTH), f32),
            pltpu.VMEM((TS, 2 * WIDTH), jnp.bfloat16),
            pltpu.VMEM((TS, 2 * WIDTH), jnp.bfloat16),
            pltpu.VMEM((N_TRIPS, TS, D_MODEL // N_TRIPS), f32),
            pltpu.VMEM((N_TRIPS, TS, D_MODEL // N_TRIPS), f32),
            pltpu.VMEM((SUBLANES, WIDTH), f32),
        ],
        compiler_params=pltpu.CompilerParams(
            dimension_semantics=("arbitrary",), vmem_limit_bytes=VMEM_LIMIT),
        name="lru_backward_out_pass",
    )(u, g, hf, ysc, x2, w_out_b, wg_b, half_row(lru_ba_b[0]), half_row(lru_bx_b[0]), row(lru_lam_b[0]),
      row(lru_norm_g[0]), row(ln_g[0]), row(ln_b[0]))

    return out.reshape(bsz, seq, d)
```

```python
import functools

import jax
import jax.numpy as jnp
from jax import lax
from jax.experimental import pallas as pl
from jax.experimental.pallas import tpu as pltpu

D_MODEL = 2048
WIDTH = 1024
HEADS = 16
HEAD_DIM = WIDTH // HEADS
PROJ_COLS = 6 * WIDTH
LRU_C = 8.0
DEEPNORM_ALPHA = 2.0 ** 0.25
LN_EPS = 1e-5
RMS_EPS = 1e-6
SQRT_FLOOR = 1e-36
PACK_ROWS = 512

SUBLANES = 8
GATE_TILE = 256
N_GATE_TILES = WIDTH // GATE_TILE
TS = 256
HALO = 16
LEAD = SUBLANES
RB = 32
PIPE_LAG = 2
N_TRIPS = 4
TRIP_ROWS = TS // N_TRIPS
SCAN_RB = 16
VMEM_LIMIT = 56 * 1024 * 1024

C_LX, C_LG, C_SB, C_SC, C_SH, C_SG = (k * WIDTH for k in range(6))


def _sigmoid(z):
    return 0.5 * jnp.tanh(0.5 * z) + 0.5


def _pack_kernel(w_ref, o_ref):
    o_ref[...] = pltpu.bitcast(w_ref[...].astype(jnp.bfloat16), jnp.uint32)


def _pack_row_pairs(w, n_slices=1):
    k, n = w.shape
    ns = n // n_slices
    kb = min(k, PACK_ROWS)
    return pl.pallas_call(
        _pack_kernel,
        grid=(n_slices, k // kb),
        in_specs=[pl.BlockSpec((kb, ns), lambda s, i: (i, s))],
        out_specs=pl.BlockSpec((None, kb // 2, ns), lambda s, i: (s, i, 0)),
        out_shape=jax.ShapeDtypeStruct((n_slices, k // 2, ns), jnp.uint32),
        name="pack_bf16_weights",
    )(w)


def _as_bf16(packed):
    return pltpu.bitcast(packed, jnp.bfloat16)


def _silu(z):
    return z * _sigmoid(z)


def _half_log_decay(lam):
    sp = jnp.maximum(-lam, 0.0) + jnp.log1p(jnp.exp(-jnp.abs(lam)))
    return (-0.5 * LRU_C) * sp


def _gate_terms(u, za_half, zx_half, hld):
    t_a = jnp.tanh(za_half)
    t_x = jnp.tanh(zx_half)
    log_a = t_a * hld + hld
    a = jnp.exp(log_a)
    m2 = jnp.tanh(log_a) * (-1.0 - a * a)
    mult = m2 * lax.rsqrt(jnp.maximum(m2, SQRT_FLOOR))
    return a, (mult * (0.5 * t_x + 0.5)) * u


def _scan8(a, b, carry, reverse):
    row = lax.broadcasted_iota(jnp.int32, a.shape, 0)
    for s in (1, 2, 4):
        if reverse:
            shift, m = SUBLANES - s, row < SUBLANES - s
        else:
            shift, m = s, row >= s
        a_sh = jnp.where(m, pltpu.roll(a, shift, 0), 1.0)
        b_sh = jnp.where(m, pltpu.roll(b, shift, 0), 0.0)
        b = a * b_sh + b
        a = a * a_sh
    return a * carry + b


def _scan_rows(a, b, carry, reverse):
    groups = list(range(a.shape[0] // SUBLANES))
    if reverse:
        groups = groups[::-1]
    out = [None] * len(groups)
    for gi in groups:
        sl = slice(gi * SUBLANES, (gi + 1) * SUBLANES)
        h = _scan8(a[sl], b[sl], carry, reverse)
        edge = h[0:1] if reverse else h[SUBLANES - 1:SUBLANES]
        carry = jnp.broadcast_to(edge, h.shape)
        out[gi] = h
    return jnp.concatenate(out, axis=0), carry


def _gate_preacts(ub_ref, wg_ref, za_ref, zx_ref, rows=slice(None)):
    for q in range(N_GATE_TILES):
        cs = slice(q * GATE_TILE, (q + 1) * GATE_TILE)
        z = jnp.dot(ub_ref[rows, cs], _as_bf16(wg_ref[q]), preferred_element_type=jnp.float32)
        za_ref[rows, cs] = z[:, :GATE_TILE]
        zx_ref[rows, cs] = z[:, GATE_TILE:]


def _pass1_kernel(x_ref, xn_ref, w_ref, cw_ref, cb_ref, wg_ref, ba_ref, bx_ref, lam_ref,
                  scw_ref, scg_ref,
                  u_out, g_out, hf_out, ysc_out,
                  xb, plx, pc, ph, pb, psg, plg, ch, ub, za, zx, hc):
    i = pl.program_id(0)
    last = pl.num_programs(0) - 1

    @pl.when(i == 0)
    def _():
        plx[0:LEAD, :] = jnp.zeros((LEAD, WIDTH), jnp.float32)
        ch[0:LEAD, :] = jnp.zeros((LEAD, WIDTH), jnp.float32)
        hc[...] = jnp.zeros_like(hc)

    @pl.when(i > 0)
    def _():
        plx[0:LEAD, :] = plx[TS:TS + LEAD, :]
        ch[0:LEAD, :] = ch[TS:TS + LEAD, :]

    keep = jnp.where(i == last, 0.0, 1.0)
    xb[0:TS, :] = x_ref[...].astype(jnp.bfloat16)
    xb[TS:TS + HALO, :] = (xn_ref[...] * keep).astype(jnp.bfloat16)

    def project(dst, lead, c0, nrows):
        dst[lead:lead + nrows, :] = jnp.dot(
            xb[0:nrows, :], _as_bf16(w_ref[c0 // WIDTH]), preferred_element_type=jnp.float32)

    def tap(ref, r0, off):
        s = LEAD + r0 + off
        return ref[s:s + RB, :]

    blocks = [k * RB for k in range(TS // RB)]

    project(plx, LEAD, C_LX, TS + HALO)
    project(pc, 0, C_SC, TS + HALO)
    project(ph, 0, C_SH, TS + HALO)
    for r0 in blocks:
        u = (tap(plx, r0, -2) * cw_ref[0:1, :] + tap(plx, r0, -1) * cw_ref[1:2, :]
             + tap(plx, r0, 0) * cw_ref[2:3, :] + tap(plx, r0, 1) * cw_ref[3:4, :]
             + cb_ref[...])
        u_out[r0:r0 + RB, :] = u
        ub[r0:r0 + RB, :] = u.astype(jnp.bfloat16)

    _gate_preacts(ub, wg_ref, za, zx)
    project(pb, 0, C_SB, TS)
    project(psg, 0, C_SG, TS)
    project(plg, 0, C_LG, TS)

    hld = _half_log_decay(lam_ref[...])
    carry = hc[...]
    for r0 in blocks:
        rows = slice(r0, r0 + RB)
        a, b = _gate_terms(u_out[rows, :], za[rows, :] + ba_ref[...], zx[rows, :] + bx_ref[...], hld)
        h, carry = _scan_rows(a, b, carry, reverse=False)
        hf_out[rows, :] = h
    hc[...] = carry

    for r0 in range(0, TS + HALO, RB // 2):
        rows = slice(r0, r0 + RB // 2)
        ch[LEAD + r0:LEAD + r0 + RB // 2, :] = pc[rows, :] * ph[rows, :]
    for r0 in blocks:
        rows = slice(r0, r0 + RB)
        v = (tap(ch, r0, -1) * scw_ref[0:1, :] + tap(ch, r0, 0) * scw_ref[1:2, :]
             + tap(ch, r0, 1) * scw_ref[2:3, :])
        y = pb[rows, :] * v * _silu(psg[rows, :])
        ms = jnp.mean(y * y, axis=-1, keepdims=True)
        ysc_out[rows, :] = (y * lax.rsqrt(ms + RMS_EPS) * scg_ref[...]).astype(ysc_out.dtype)

    for r0 in blocks:
        g_out[r0:r0 + RB, :] = _silu(plg[r0:r0 + RB, :])


def _pass2_kernel(u_ref, g_ref, hf_ref, ysc_ref, x_ref, wo_ref, wg_ref, ba_ref, bx_ref, lam_ref,
                  lg_ref, lng_ref, lnb_ref,
                  o_ref,
                  ub, za, zx, ycat_a, ycat_b, mixed_a, mixed_b, hc):
    i = pl.program_id(0)

    @pl.when(i == 0)
    def _():
        hc[...] = jnp.zeros_like(hc)
        ycat_b[...] = jnp.zeros_like(ycat_b)
        mixed_a[...] = jnp.zeros_like(mixed_a)


    def step(ycat_cur, ycat_prv, mixed_cur, mixed_prv):
        ub[...] = u_ref[...].astype(jnp.bfloat16)
        _gate_preacts(ub, wg_ref, za, zx)

        for r0 in range(0, TS, RB):
            rows = slice(r0, r0 + RB)
            m = jnp.concatenate([mixed_cur[k, rows, :] for k in range(N_TRIPS)], axis=-1)
            r = DEEPNORM_ALPHA * x_ref[rows, :] + m
            mu = jnp.mean(r, axis=-1, keepdims=True)
            d = r - mu
            var = jnp.mean(d * d, axis=-1, keepdims=True)
            o_ref[rows, :] = d * lax.rsqrt(var + LN_EPS) * lng_ref[...] + lnb_ref[...]

        hld = _half_log_decay(lam_ref[...])

        def body(k, carry):
            base = pl.multiple_of((N_TRIPS - 1 - k) * TRIP_ROWS, TRIP_ROWS)
            for off in range(TRIP_ROWS - SCAN_RB, -1, -SCAN_RB):
                rows = pl.ds(pl.multiple_of(base + off, SCAN_RB), SCAN_RB)
                a, b = _gate_terms(u_ref[rows, :], za[rows, :] + ba_ref[...], zx[rows, :] + bx_ref[...], hld)
                hb, carry = _scan_rows(a, b, carry, reverse=True)
                y = (hf_ref[rows, :] + hb) * g_ref[rows, :]
                ms = jnp.mean(y * y, axis=-1, keepdims=True)
                ycat_cur[rows, 0:WIDTH] = (y * lax.rsqrt(ms + RMS_EPS) * lg_ref[...]).astype(jnp.bfloat16)
                ycat_cur[rows, WIDTH:] = ysc_ref[rows, :]
            mixed_prv[k] = jnp.dot(ycat_prv[...], _as_bf16(wo_ref[k]),
                                   preferred_element_type=jnp.float32)
            return carry

        hc[...] = lax.fori_loop(0, N_TRIPS, body, hc[...])

    pl.when(i % 2 == 0)(functools.partial(step, ycat_a, ycat_b, mixed_a, mixed_b))
    pl.when(i % 2 == 1)(functools.partial(step, ycat_b, ycat_a, mixed_b, mixed_a))


def _block_diag_tiles(w):
    per = GATE_TILE // HEAD_DIM
    w4 = w.reshape(N_GATE_TILES, per, HEAD_DIM, HEAD_DIM)
    eye = jnp.eye(per, dtype=w.dtype)
    return jnp.einsum('qhij,hk->qhikj', w4, eye).reshape(N_GATE_TILES, GATE_TILE, GATE_TILE)


def _pack_gates(wa, wx):
    tiles = 0.5 * jnp.concatenate([_block_diag_tiles(wa), _block_diag_tiles(wx)], axis=-1)
    packed = _pack_row_pairs(tiles.reshape(WIDTH, 2 * GATE_TILE))
    return packed.reshape(N_GATE_TILES, GATE_TILE // 2, 2 * GATE_TILE)


def kernel(x, w_in, lru_conv_w, lru_conv_b, lru_wa_f, lru_ba_f, lru_wx_f, lru_bx_f, lru_lam_f,
           lru_wa_b, lru_ba_b, lru_wx_b, lru_bx_b, lru_lam_b, sc_conv_w, lru_norm_g, sc_norm_g,
           w_out, ln_g, ln_b):
    bsz, seq, d = x.shape
    assert bsz == 1 and d == D_MODEL and seq % TS == 0 and w_in.shape == (1, D_MODEL, PROJ_COLS)
    n_steps = seq // TS
    x2 = x.reshape(seq, d)
    row = lambda v: v.reshape(1, -1)

    half_row = lambda v: 0.5 * v.reshape(1, -1)
    w_in_b = _pack_row_pairs(w_in[0], PROJ_COLS // WIDTH)
    w_out_b = _pack_row_pairs(w_out[0], N_TRIPS)
    wg_f = _pack_gates(lru_wa_f[0], lru_wx_f[0])
    wg_b = _pack_gates(lru_wa_b[0], lru_wx_b[0])

    chunk = lambda w: pl.BlockSpec((TS, w), lambda i: (i, 0))
    halo_blocks = seq // HALO
    halo_spec = pl.BlockSpec(
        (HALO, D_MODEL), lambda i: (jnp.minimum((i + 1) * (TS // HALO), halo_blocks - 1), 0))
    whole = pl.BlockSpec(memory_space=pltpu.VMEM)
    f32 = jnp.float32

    u, g, hf, ysc = pl.pallas_call(
        _pass1_kernel,
        grid=(n_steps,),
        in_specs=[chunk(D_MODEL), halo_spec] + [whole] * 9,
        out_specs=[chunk(WIDTH)] * 4,
        out_shape=[jax.ShapeDtypeStruct((seq, WIDTH), f32)] * 3
        + [jax.ShapeDtypeStruct((seq, WIDTH), jnp.bfloat16)],
        scratch_shapes=[
            pltpu.VMEM((TS + HALO, D_MODEL), jnp.bfloat16),
            pltpu.VMEM((LEAD + TS + HALO, WIDTH), f32),
            pltpu.VMEM((TS + HALO, WIDTH), f32),
            pltpu.VMEM((TS + HALO, WIDTH), f32),
            pltpu.VMEM((TS, WIDTH), f32),
            pltpu.VMEM((TS, WIDTH), f32),
            pltpu.VMEM((TS, WIDTH), f32),
            pltpu.VMEM((LEAD + TS + HALO, WIDTH), f32),
            pltpu.VMEM((TS, WIDTH), jnp.bfloat16),
            pltpu.VMEM((TS, WIDTH), f32),
            pltpu.VMEM((TS, WIDTH), f32),
            pltpu.VMEM((SUBLANES, WIDTH), f32),
        ],
        compiler_params=pltpu.CompilerParams(
            dimension_semantics=("arbitrary",), vmem_limit_bytes=VMEM_LIMIT),
        name="lru_sc_forward_pass",
    )(x2, x2, w_in_b, lru_conv_w[0], row(lru_conv_b[0]), wg_f, half_row(lru_ba_f[0]), half_row(lru_bx_f[0]),
      row(lru_lam_f[0]), sc_conv_w[0], row(sc_norm_g[0]))

    scan_chunk = lambda w: pl.BlockSpec((TS, w), lambda i: (jnp.maximum(n_steps - 1 - i, 0), 0))
    done_chunk = pl.BlockSpec(
        (TS, D_MODEL), lambda i: (jnp.minimum(n_steps - 1 + PIPE_LAG - i, n_steps - 1), 0))
    out = pl.pallas_call(
        _pass2_kernel,
        grid=(n_steps + PIPE_LAG,),
        in_specs=[scan_chunk(WIDTH)] * 4 + [done_chunk] + [whole] * 8,
        out_specs=done_chunk,
        out_shape=jax.ShapeDtypeStruct((seq, D_MODEL), x.dtype),
        scratch_shapes=[
            pltpu.VMEM((TS, WIDTH), jnp.bfloat16),
            pltpu.VMEM((TS, WIDTH), f32),
            pltpu.VMEM((TS, WIDTH), f32),
            pltpu.VMEM((TS, 2 * WIDTH), jnp.bfloat16),
            pltpu.VMEM((TS, 2 * WIDTH), jnp.bfloat16),
            pltpu.VMEM((N_TRIPS, TS, D_MODEL // N_TRIPS), f32),
            pltpu.VMEM((N_TRIPS, TS, D_MODEL // N_TRIPS), f32),
            pltpu.VMEM((SUBLANES, WIDTH), f32),
        ],
        compiler_params=pltpu.CompilerParams(
            dimension_semantics=("arbitrary",), vmem_limit_bytes=VMEM_LIMIT),
        name="lru_backward_out_pass",
    )(u, g, hf, ysc, x2, w_out_b, wg_b, half_row(lru_ba_b[0]), half_row(lru_bx_b[0]), row(lru_lam_b[0]),
      row(lru_norm_g[0]), row(ln_g[0]), row(ln_b[0]))

    return out.reshape(bsz, seq, d)
```
